```python
import jax, jax.numpy as jnp
from jax import lax
import numpy as np

D_MODEL = 1024
BATCH = 8
SEQ = 2048
DEPTH = 2

CHUNK = 64
HEAD_DIM = 64
N_HEADS_A = 8
N_HEADS_B = 8
D_A = N_HEADS_A * HEAD_DIM
D_B = N_HEADS_B * HEAD_DIM
D_MIX = D_A + D_B
IN_COLS = 2 * D_A + 3 * D_B
GMLP_BLOCK = 128
ATT_LEFT_CHUNKS = 8
ATT_BAND = (ATT_LEFT_CHUNKS + 1) * CHUNK
REL_CLIP = 128
N_REL = 2 * REL_CLIP + 1
CONV_WIDTH = 31
D_CONV = D_MODEL
D_FF = ((-(-8 * D_MODEL // 3)) + 255) // 256 * 256
N_EVEN = (DEPTH + 1) // 2
N_ODD = DEPTH // 2
EPS = 1e-6
NEG_INF = -1e30

kernel_name = "hybrid_gmlp_relattn_conformerconv_encoder"


def _rmsnorm(x, g):
    xf = x.astype(jnp.float32)
    y = xf * lax.rsqrt(jnp.mean(xf * xf, axis=-1, keepdims=True) + EPS)
    return (y * g.astype(jnp.float32)).astype(x.dtype)


def _layernorm(x, g, b):
    xf = x.astype(jnp.float32)
    mu = jnp.mean(xf, axis=-1, keepdims=True)
    var = jnp.mean(jnp.square(xf - mu), axis=-1, keepdims=True)
    y = (xf - mu) * lax.rsqrt(var + EPS)
    return (y * g.astype(jnp.float32) + b.astype(jnp.float32)).astype(x.dtype)


def _mixer_a(u, v, ln_g, ln_b, w_s, b_s):
    bsz, seq, _ = u.shape
    nb = seq // GMLP_BLOCK
    u = jax.nn.gelu(u)
    v = _layernorm(jax.nn.gelu(v), ln_g, ln_b)
    v = v.reshape(bsz, nb, GMLP_BLOCK, N_HEADS_A, HEAD_DIM)
    pos_chunk = jnp.arange(GMLP_BLOCK) // CHUNK
    mask = (pos_chunk[None, :] <= pos_chunk[:, None]).astype(w_s.dtype)
    w = w_s * mask[None]
    gate = jnp.einsum('hij,bnjhd->bnihd', w, v) + b_s.T[None, None, :, :, None]
    return u * gate.reshape(bsz, seq, D_A)


def _mixer_b(q, k, v, rel_bias):
    bsz, seq, _ = q.shape
    nc = seq // CHUNK
    shp = (bsz, nc, CHUNK, N_HEADS_B, HEAD_DIM)
    q, k, v = q.reshape(shp), k.reshape(shp), v.reshape(shp)
    pad = ((0, 0), (ATT_LEFT_CHUNKS, 0), (0, 0), (0, 0), (0, 0))
    kp, vp = jnp.pad(k, pad), jnp.pad(v, pad)

    def band(t):
        return jnp.concatenate([t[:, w:w + nc] for w in range(ATT_LEFT_CHUNKS + 1)], axis=2)

    kb, vb = band(kp), band(vp)
    s = jnp.einsum('bnqhd,bnkhd->bhnqk', q, kb).astype(jnp.float32) * (HEAD_DIM ** -0.5)
    qi = jnp.arange(CHUNK)
    kj = jnp.arange(ATT_BAND)
    rel = ATT_LEFT_CHUNKS * CHUNK + qi[:, None] - kj[None, :]
    rel_idx = jnp.clip(rel, -REL_CLIP, REL_CLIP) + REL_CLIP
    bias = rel_bias[:, rel_idx].astype(jnp.float32)
    key_chunk = jnp.arange(nc)[:, None] - ATT_LEFT_CHUNKS + (kj // CHUNK)[None, :]
    valid = key_chunk >= 0
    s = jnp.where(valid[None, None, :, None, :], s + bias[None, :, None], NEG_INF)
    p = jax.nn.softmax(s, axis=-1).astype(vb.dtype)
    o = jnp.einsum('bhnqk,bnkhd->bnqhd', p, vb)
    return o.reshape(bsz, seq, D_B)


def _conv_module(h, w1, b1, w_dw, b_dw, ln_g, ln_b, w2, b2):
    z = h @ w1 + b1
    a, g = jnp.split(z, 2, axis=-1)
    z = a * jax.nn.sigmoid(g)
    zp = jnp.pad(z, ((0, 0), (CONV_WIDTH - 1, 0), (0, 0)))
    z = lax.conv_general_dilated(
        zp, w_dw[:, None, :], window_strides=(1,), padding='VALID',
        dimension_numbers=('NWC', 'WIO', 'NWC'), feature_group_count=D_CONV) + b_dw
    z = jax.nn.silu(_layernorm(z, ln_g, ln_b))
    return z @ w2 + b2


def _swiglu(h, wg, wu, wd):
    return (jax.nn.silu(h @ wg) * (h @ wu)) @ wd


def setup_inputs(seed: int = 0) -> dict:
    key = jax.random.key(seed)
    ks = jax.random.split(key, 24)
    f32 = jnp.float32
    nrm = lambda k, shape, scale: jax.random.normal(k, shape, f32) * scale
    return {
        "x": nrm(ks[0], (BATCH, SEQ, D_MODEL), 1.0),
        "w_in": nrm(ks[1], (N_EVEN, D_MODEL, IN_COLS), D_MODEL ** -0.5),
        "w_out": nrm(ks[2], (N_EVEN, D_MIX, D_MODEL), D_MIX ** -0.5),
        "a_ln_g": 1.0 + nrm(ks[3], (N_EVEN, D_A), 0.05),
        "a_ln_b": nrm(ks[4], (N_EVEN, D_A), 0.02),
        "a_ws": nrm(ks[5], (N_EVEN, N_HEADS_A, GMLP_BLOCK, GMLP_BLOCK), GMLP_BLOCK ** -0.5),
        "a_bs": 1.0 + nrm(ks[6], (N_EVEN, N_HEADS_A, GMLP_BLOCK), 0.1),
        "b_rel": nrm(ks[7], (N_EVEN, N_HEADS_B, N_REL), 0.2),
        "c_w1": nrm(ks[8], (N_ODD, D_MODEL, 2 * D_CONV), D_MODEL ** -0.5),
        "c_b1": nrm(ks[9], (N_ODD, 2 * D_CONV), 0.02),
        "c_dw": nrm(ks[10], (N_ODD, CONV_WIDTH, D_CONV), CONV_WIDTH ** -0.5),
        "c_bdw": nrm(ks[11], (N_ODD, D_CONV), 0.02),
        "c_ln_g": 1.0 + nrm(ks[12], (N_ODD, D_CONV), 0.05),
        "c_ln_b": nrm(ks[13], (N_ODD, D_CONV), 0.02),
        "c_w2": nrm(ks[14], (N_ODD, D_CONV, D_MODEL), D_CONV ** -0.5),
        "c_b2": nrm(ks[15], (N_ODD, D_MODEL), 0.02),
        "g_mix": 1.0 + nrm(ks[16], (DEPTH, D_MODEL), 0.05),
        "g_ffn": 1.0 + nrm(ks[17], (DEPTH, D_MODEL), 0.05),
        "f_wg": nrm(ks[18], (DEPTH, D_MODEL, D_FF), D_MODEL ** -0.5),
        "f_wu": nrm(ks[19], (DEPTH, D_MODEL, D_FF), D_MODEL ** -0.5),
        "f_wd": nrm(ks[20], (DEPTH, D_FF, D_MODEL), D_FF ** -0.5),
        "g_final": 1.0 + nrm(ks[21], (D_MODEL,), 0.05),
    }


def reference(x, w_in, w_out, a_ln_g, a_ln_b, a_ws, a_bs, b_rel,
              c_w1, c_b1, c_dw, c_bdw, c_ln_g, c_ln_b, c_w2, c_b2,
              g_mix, g_ffn, f_wg, f_wu, f_wd, g_final):
    split_pts = [D_A, 2 * D_A, 2 * D_A + D_B, 2 * D_A + 2 * D_B]
    for layer in range(DEPTH):
        i = layer // 2
        h = _rmsnorm(x, g_mix[layer])
        if layer % 2 == 0:
            proj = h @ w_in[i]
            u_a, v_a, q_b, k_b, v_b = jnp.split(proj, split_pts, axis=-1)
            y_a = _mixer_a(u_a, v_a, a_ln_g[i], a_ln_b[i], a_ws[i], a_bs[i])
            y_b = _mixer_b(q_b, k_b, v_b, b_rel[i])
            x = x + jnp.concatenate([y_a, y_b], axis=-1) @ w_out[i]
        else:
            x = x + _conv_module(h, c_w1[i], c_b1[i], c_dw[i], c_bdw[i],
                                 c_ln_g[i], c_ln_b[i], c_w2[i], c_b2[i])
        h = _rmsnorm(x, g_ffn[layer])
        x = x + _swiglu(h, f_wg[layer], f_wu[layer], f_wd[layer])
    return _rmsnorm(x, g_final)
```

```python
import functools

import jax
import jax.numpy as jnp
from jax import lax
from jax.experimental import pallas as pl
from jax.experimental.pallas import tpu as pltpu

D_MODEL = 1024
CHUNK = 64
HEAD_DIM = 64
N_HEADS_A = 8
N_HEADS_B = 8
D_A = N_HEADS_A * HEAD_DIM
D_B = N_HEADS_B * HEAD_DIM
GMLP_BLOCK = 128
ATT_LEFT_CHUNKS = 8
ATT_HALO = ATT_LEFT_CHUNKS * CHUNK
ATT_BAND = ATT_HALO + CHUNK
REL_CLIP = 128
CONV_WIDTH = 31
EPS = 1e-6
NEG_INF = -1e30

LANES = 128
HEADS_PER_TILE = LANES // HEAD_DIM
N_PAIRS_A = D_A // LANES
N_PAIRS_B = D_B // LANES

SEQ_TILE = 512
TOKEN_TILE = 512
FF_CHUNK = 256
CONV_HALO = 32
CONV_ROWS = 64
CONV_LANES = 256
SUBLANES = 8
VMEM_LIMIT_BYTES = 56 * 1024 * 1024

BF16 = jnp.bfloat16
F32 = jnp.float32


def _dot(a, b):
    return jnp.dot(a, b, preferred_element_type=F32)


def _dot_nt(a, b):
    return lax.dot_general(a, b, (((1,), (1,)), ((), ())), preferred_element_type=F32)


def _rmsnorm(x, g):
    return x * lax.rsqrt(jnp.mean(x * x, axis=-1, keepdims=True) + EPS) * g


def _layernorm(x, g, b):
    mu = jnp.mean(x, axis=-1, keepdims=True)
    xc = x - mu
    var = jnp.mean(xc * xc, axis=-1, keepdims=True)
    return xc * lax.rsqrt(var + EPS) * g + b


def _const_spec(shape):
    return pl.BlockSpec(shape, lambda *_: (0,) * len(shape), pipeline_mode=pl.Buffered(1))


def _mixer_even_kernel(x_ref, g_ref, w_in_ref, w_out_ref, ln_g_ref, ln_b_ref, ws_ref, bs_ref,
                       bias_ref, o_ref,
                       h_scr, u_scr, va_scr, q0_scr, q1_scr, k_scr, v_scr, y_scr):
    t = pl.program_id(1)
    ts = x_ref.shape[0]

    x = x_ref[...]
    h_scr[...] = _rmsnorm(x, g_ref[...]).astype(BF16)

    lane = lax.broadcasted_iota(jnp.int32, (1, D_B), 1)
    first_head = (lane % LANES) < HEAD_DIM

    u_scr[...] = jax.nn.gelu(_dot(h_scr[...], w_in_ref[:, 0:D_A]))
    va = jax.nn.gelu(_dot(h_scr[...], w_in_ref[:, D_A:2 * D_A]))
    va_scr[...] = _layernorm(va, ln_g_ref[...], ln_b_ref[...]).astype(BF16)
    q = (_dot(h_scr[...], w_in_ref[:, 2 * D_A:2 * D_A + D_B]) * (HEAD_DIM ** -0.5)).astype(BF16)
    zero = jnp.zeros_like(q)
    q0_scr[...] = jnp.where(first_head, q, zero)
    q1_scr[...] = jnp.where(first_head, zero, q)

    @pl.when(t == 0)
    def _():
        k_scr[0:ATT_HALO, :] = jnp.zeros((ATT_HALO, D_B), BF16)
        v_scr[0:ATT_HALO, :] = jnp.zeros((ATT_HALO, D_B), BF16)

    k_scr[ATT_HALO:ATT_HALO + ts, :] = _dot(
        h_scr[...], w_in_ref[:, 2 * D_A + D_B:2 * D_A + 2 * D_B]).astype(BF16)
    v_scr[ATT_HALO:ATT_HALO + ts, :] = _dot(
        h_scr[...], w_in_ref[:, 2 * D_A + 2 * D_B:2 * D_A + 3 * D_B]).astype(BF16)

    row_chunk = lax.broadcasted_iota(jnp.int32, (GMLP_BLOCK, GMLP_BLOCK), 0) // CHUNK
    col_chunk = lax.broadcasted_iota(jnp.int32, (GMLP_BLOCK, GMLP_BLOCK), 1) // CHUNK
    causal = (col_chunk <= row_chunk).astype(F32)
    w_sp = [(ws_ref[hd] * causal).astype(BF16) for hd in range(N_HEADS_A)]
    first_head_tile = first_head[:, 0:LANES]
    for blk in range(ts // GMLP_BLOCK):
        rows = slice(blk * GMLP_BLOCK, (blk + 1) * GMLP_BLOCK)
        for p in range(N_PAIRS_A):
            cols = slice(p * LANES, (p + 1) * LANES)
            vp = va_scr[rows, cols]
            gate = jnp.where(first_head_tile,
                             _dot(w_sp[HEADS_PER_TILE * p], vp),
                             _dot(w_sp[HEADS_PER_TILE * p + 1], vp)) + bs_ref[:, cols]
            y_scr[rows, cols] = (u_scr[rows, cols] * gate).astype(BF16)

    key_pos = lax.broadcasted_iota(jnp.int32, (1, ATT_BAND), 1)
    chunks_per_tile = ts // CHUNK

    def chunk_body(c, carry):
        r0 = pl.multiple_of(c * CHUNK, CHUNK)
        first_valid_key = (ATT_LEFT_CHUNKS - (t * chunks_per_tile + c)) * CHUNK
        valid = key_pos >= first_valid_key
        for p in range(N_PAIRS_B):
            cols = slice(p * LANES, (p + 1) * LANES)
            qm = jnp.concatenate([q0_scr[pl.ds(r0, CHUNK), cols], q1_scr[pl.ds(r0, CHUNK), cols]], axis=0)
            kp = k_scr[pl.ds(r0, ATT_BAND), cols]
            vp = v_scr[pl.ds(r0, ATT_BAND), cols]
            s = jnp.where(valid, _dot_nt(qm, kp) + bias_ref[p], NEG_INF)
            e = jnp.exp(s - jnp.max(s, axis=-1, keepdims=True))
            o = _dot(e.astype(BF16), vp) / jnp.sum(e, axis=-1, keepdims=True)
            y_scr[pl.ds(r0, CHUNK), D_A + p * LANES:D_A + (p + 1) * LANES] = jnp.where(
                first_head_tile, o[0:CHUNK], o[CHUNK:2 * CHUNK]).astype(BF16)
        return carry

    lax.fori_loop(0, chunks_per_tile, chunk_body, 0)

    k_scr[0:ATT_HALO, :] = k_scr[ts:ts + ATT_HALO, :]
    v_scr[0:ATT_HALO, :] = v_scr[ts:ts + ATT_HALO, :]

    o_ref[...] = x_ref[...] + _dot(y_scr[...], w_out_ref[...])


def _attention_bias(b_rel):
    qi = jnp.arange(CHUNK)
    kj = jnp.arange(ATT_BAND)
    rel = ATT_HALO + qi[:, None] - kj[None, :]
    idx = jnp.clip(rel, -REL_CLIP, REL_CLIP) + REL_CLIP
    bias = b_rel[:, idx].astype(F32)
    return bias.reshape(N_PAIRS_B, HEADS_PER_TILE * CHUNK, ATT_BAND)


def _mixer_even(x, g, w_in, w_out, ln_g, ln_b, w_s, b_s, b_rel):
    bsz, seq, d = x.shape
    ts = SEQ_TILE
    assert seq % ts == 0 and ts % GMLP_BLOCK == 0 and ts >= ATT_HALO
    in_cols = w_in.shape[1]
    bs_rows = jnp.repeat(b_s.T, HEAD_DIM, axis=1).astype(F32)
    bias = _attention_bias(b_rel)
    x_spec = pl.BlockSpec((None, ts, d), lambda b, t: (b, t, 0))
    return pl.pallas_call(
        _mixer_even_kernel,
        grid=(bsz, seq // ts),
        in_specs=[
            x_spec,
            _const_spec((1, d)),
            _const_spec((d, in_cols)),
            _const_spec((D_A + D_B, d)),
            _const_spec((1, D_A)),
            _const_spec((1, D_A)),
            _const_spec((N_HEADS_A, GMLP_BLOCK, GMLP_BLOCK)),
            _const_spec((GMLP_BLOCK, D_A)),
            _const_spec((N_PAIRS_B, HEADS_PER_TILE * CHUNK, ATT_BAND)),
        ],
        out_specs=x_spec,
        out_shape=jax.ShapeDtypeStruct(x.shape, x.dtype),
        scratch_shapes=[
            pltpu.VMEM((ts, d), BF16),
            pltpu.VMEM((ts, D_A), F32),
            pltpu.VMEM((ts, D_A), BF16),
            pltpu.VMEM((ts, D_B), BF16),
            pltpu.VMEM((ts, D_B), BF16),
            pltpu.VMEM((ATT_HALO + ts, D_B), BF16),
            pltpu.VMEM((ATT_HALO + ts, D_B), BF16),
            pltpu.VMEM((ts, D_A + D_B), BF16),
        ],
        compiler_params=pltpu.CompilerParams(
            dimension_semantics=("arbitrary", "arbitrary"), vmem_limit_bytes=VMEM_LIMIT_BYTES),
        name="mixer_even",
    )(x, g.reshape(1, d), w_in.astype(BF16), w_out.astype(BF16), ln_g.reshape(1, D_A),
      ln_b.reshape(1, D_A), w_s, bs_rows, bias)


def _mixer_odd_kernel(x_ref, g_ref, w1_ref, b1_ref, wdw_ref, bdw_ref, ln_g_ref, ln_b_ref, w2_ref,
                      b2_ref, o_ref, h_scr, z_scr, c_scr, a_scr):
    t = pl.program_id(1)
    ts, d = x_ref.shape

    h_scr[...] = _rmsnorm(x_ref[...], g_ref[...]).astype(BF16)

    @pl.when(t == 0)
    def _():
        z_scr[0:CONV_HALO, :] = jnp.zeros((CONV_HALO, d), F32)

    a = _dot(h_scr[...], w1_ref[:, 0:d]) + b1_ref[:, 0:d]
    gate = _dot(h_scr[...], w1_ref[:, d:2 * d]) + b1_ref[:, d:2 * d]
    z_scr[CONV_HALO:CONV_HALO + ts, :] = a * jax.nn.sigmoid(gate)

    first_tap = CONV_HALO - (CONV_WIDTH - 1)
    for lt in range(d // CONV_LANES):
        cols = slice(lt * CONV_LANES, (lt + 1) * CONV_LANES)
        for rb in range(ts // CONV_ROWS):
            r0 = rb * CONV_ROWS
            acc = jnp.zeros((CONV_ROWS, CONV_LANES), F32)
            for phase in range(SUBLANES):
                offs = [first_tap + k for k in range(CONV_WIDTH) if (first_tap + k) % SUBLANES == phase]
                window = z_scr[r0 + offs[0]:r0 + offs[-1] + CONV_ROWS, cols]
                for off in offs:
                    k = off - first_tap
                    acc = acc + window[off - offs[0]:off - offs[0] + CONV_ROWS] * wdw_ref[k:k + 1, cols]
            c_scr[r0:r0 + CONV_ROWS, cols] = acc + bdw_ref[:, cols]

    z_scr[0:CONV_HALO, :] = z_scr[ts:ts + CONV_HALO, :]

    zc = _layernorm(c_scr[...], ln_g_ref[...], ln_b_ref[...])
    a_scr[...] = (zc * jax.nn.sigmoid(zc)).astype(BF16)

    o_ref[...] = x_ref[...] + _dot(a_scr[...], w2_ref[...]) + b2_ref[...]


def _mixer_odd(x, g, w1, b1, w_dw, b_dw, ln_g, ln_b, w2, b2):
    bsz, seq, d = x.shape
    ts = SEQ_TILE
    assert seq % ts == 0 and ts % CONV_ROWS == 0 and CONV_HALO >= CONV_WIDTH - 1
    x_spec = pl.BlockSpec((None, ts, d), lambda b, t: (b, t, 0))
    return pl.pallas_call(
        _mixer_odd_kernel,
        grid=(bsz, seq // ts),
        in_specs=[
            x_spec,
            _const_spec((1, d)),
            _const_spec((d, 2 * d)),
            _const_spec((1, 2 * d)),
            _const_spec((CONV_WIDTH, d)),
            _const_spec((1, d)),
            _const_spec((1, d)),
            _const_spec((1, d)),
            _const_spec((d, d)),
            _const_spec((1, d)),
        ],
        out_specs=x_spec,
        out_shape=jax.ShapeDtypeStruct(x.shape, x.dtype),
        scratch_shapes=[
            pltpu.VMEM((ts, d), BF16),
            pltpu.VMEM((CONV_HALO + ts, d), F32),
            pltpu.VMEM((ts, d), F32),
            pltpu.VMEM((ts, d), BF16),
        ],
        compiler_params=pltpu.CompilerParams(
            dimension_semantics=("arbitrary", "arbitrary"), vmem_limit_bytes=VMEM_LIMIT_BYTES),
        name="mixer_odd",
    )(x, g.reshape(1, d), w1.astype(BF16), b1.reshape(1, 2 * d), w_dw, b_dw.reshape(1, d),
      ln_g.reshape(1, d), ln_b.reshape(1, d), w2.astype(BF16), b2.reshape(1, d))


def _ffn_kernel(x_ref, g_ref, wg_ref, wu_ref, wd_ref, *rest, final_norm):
    if final_norm:
        gf_ref, o_ref, h_scr, a_scr = rest
    else:
        o_ref, h_scr, a_scr = rest
    d_ff = wg_ref.shape[1]

    h_scr[...] = _rmsnorm(x_ref[...], g_ref[...]).astype(BF16)
    for j in range(d_ff // FF_CHUNK):
        cols = slice(j * FF_CHUNK, (j + 1) * FF_CHUNK)
        gate = _dot(h_scr[...], wg_ref[:, cols])
        up = _dot(h_scr[...], wu_ref[:, cols])
        a_scr[:, cols] = (gate * jax.nn.sigmoid(gate) * up).astype(BF16)
    y = x_ref[...] + _dot(a_scr[...], wd_ref[...])
    if final_norm:
        y = _rmsnorm(y, gf_ref[...])
    o_ref[...] = y


def _ffn(x, g, wg, wu, wd, g_final=None):
    bsz, seq, d = x.shape
    d_ff = wg.shape[1]
    n_tok = bsz * seq
    tm = TOKEN_TILE
    assert n_tok % tm == 0 and d_ff % FF_CHUNK == 0
    x_spec = pl.BlockSpec((tm, d), lambda i: (i, 0))
    in_specs = [x_spec, _const_spec((1, d)), _const_spec((d, d_ff)), _const_spec((d, d_ff)),
                _const_spec((d_ff, d))]
    args = [x.reshape(n_tok, d), g.reshape(1, d), wg.astype(BF16), wu.astype(BF16), wd.astype(BF16)]
    if g_final is not None:
        in_specs.append(_const_spec((1, d)))
        args.append(g_final.reshape(1, d))
    out = pl.pallas_call(
        functools.partial(_ffn_kernel, final_norm=g_final is not None),
        grid=(n_tok // tm,),
        in_specs=in_specs,
        out_specs=x_spec,
        out_shape=jax.ShapeDtypeStruct((n_tok, d), x.dtype),
        scratch_shapes=[
            pltpu.VMEM((tm, d), BF16),
            pltpu.VMEM((tm, d_ff), BF16),
        ],
        compiler_params=pltpu.CompilerParams(
            dimension_semantics=("parallel",), vmem_limit_bytes=VMEM_LIMIT_BYTES),
        name="ffn_final" if g_final is not None else "ffn",
    )(*args)
    return out.reshape(bsz, seq, d)


def kernel(x, w_in, w_out, a_ln_g, a_ln_b, a_ws, a_bs, b_rel, c_w1, c_b1, c_dw, c_bdw, c_ln_g, c_ln_b,
           c_w2, c_b2, g_mix, g_ffn, f_wg, f_wu, f_wd, g_final):
    depth = g_mix.shape[0]
    for layer in range(depth):
        i = layer // 2
        if layer % 2 == 0:
            x = _mixer_even(x, g_mix[layer], w_in[i], w_out[i], a_ln_g[i], a_ln_b[i], a_ws[i], a_bs[i],
                            b_rel[i])
        else:
            x = _mixer_odd(x, g_mix[layer], c_w1[i], c_b1[i], c_dw[i], c_bdw[i], c_ln_g[i], c_ln_b[i],
                           c_w2[i], c_b2[i])
        x = _ffn(x, g_ffn[layer], f_wg[layer], f_wu[layer], f_wd[layer],
                 g_final if layer == depth - 1 else None)
    return x
```

```python
import functools

import jax
import jax.numpy as jnp
import numpy as np
from jax import lax
from jax.experimental import pallas as pl
from jax.experimental.pallas import tpu as pltpu

D_MODEL = 1024
CHUNK = 64
HEAD_DIM = 64
N_HEADS_A = 8
N_HEADS_B = 8
D_A = N_HEADS_A * HEAD_DIM
D_B = N_HEADS_B * HEAD_DIM
GMLP_BLOCK = 128
ATT_LEFT_CHUNKS = 8
ATT_HALO = ATT_LEFT_CHUNKS * CHUNK
ATT_BAND = ATT_HALO + CHUNK
REL_CLIP = 128
CONV_WIDTH = 31
EPS = 1e-6
NEG_INF = -1e30

LANES = 128
HEADS_PER_TILE = LANES // HEAD_DIM
N_PAIRS_A = D_A // LANES
N_PAIRS_B = D_B // LANES

SEQ_TILE = 512
TOKEN_TILE = 512
FF_CHUNK = 256
SUBLANES = 8
CONV_GROUP = SEQ_TILE // SUBLANES
CONV_PITCH = CONV_GROUP + SUBLANES
CONV_OUT_BLOCK = 8
GLU_COLS = 256
VMEM_LIMIT_BYTES = 56 * 1024 * 1024

BF16 = jnp.bfloat16
F32 = jnp.float32


def _dot(a, b):
    return jnp.dot(a, b, preferred_element_type=F32)


def _dot_nt(a, b):
    return lax.dot_general(a, b, (((1,), (1,)), ((), ())), preferred_element_type=F32)


def _rmsnorm(x, g):
    return x * lax.rsqrt(jnp.mean(x * x, axis=-1, keepdims=True) + EPS) * g


def _layernorm(x, g, b):
    mu = jnp.mean(x, axis=-1, keepdims=True)
    xc = x - mu
    var = jnp.mean(xc * xc, axis=-1, keepdims=True)
    return xc * lax.rsqrt(var + EPS) * g + b


def _const_spec(shape):
    return pl.BlockSpec(shape, lambda *_: (0,) * len(shape), pipeline_mode=pl.Buffered(1))


def _mixer_even_kernel(x_ref, g_ref, w_in_ref, w_out_ref, ln_g_ref, ln_b_ref, ws_ref, bs_ref,
                       bias_ref, o_ref,
                       h_scr, u_scr, va_scr, q0_scr, q1_scr, k_scr, v_scr, y_scr):
    t = pl.program_id(1)
    ts = x_ref.shape[0]

    x = x_ref[...]
    h_scr[...] = _rmsnorm(x, g_ref[...]).astype(BF16)

    lane = lax.broadcasted_iota(jnp.int32, (1, D_B), 1)
    first_head = (lane % LANES) < HEAD_DIM

    u_scr[...] = jax.nn.gelu(_dot(h_scr[...], w_in_ref[:, 0:D_A]))
    va = jax.nn.gelu(_dot(h_scr[...], w_in_ref[:, D_A:2 * D_A]))
    va_scr[...] = _layernorm(va, ln_g_ref[...], ln_b_ref[...]).astype(BF16)
    q = (_dot(h_scr[...], w_in_ref[:, 2 * D_A:2 * D_A + D_B]) * (HEAD_DIM ** -0.5)).astype(BF16)
    zero = jnp.zeros_like(q)
    q0_scr[...] = jnp.where(first_head, q, zero)
    q1_scr[...] = jnp.where(first_head, zero, q)

    @pl.when(t == 0)
    def _():
        k_scr[0:ATT_HALO, :] = jnp.zeros((ATT_HALO, D_B), BF16)
        v_scr[0:ATT_HALO, :] = jnp.zeros((ATT_HALO, D_B), BF16)

    k_scr[ATT_HALO:ATT_HALO + ts, :] = _dot(
        h_scr[...], w_in_ref[:, 2 * D_A + D_B:2 * D_A + 2 * D_B]).astype(BF16)
    v_scr[ATT_HALO:ATT_HALO + ts, :] = _dot(
        h_scr[...], w_in_ref[:, 2 * D_A + 2 * D_B:2 * D_A + 3 * D_B]).astype(BF16)

    row_chunk = lax.broadcasted_iota(jnp.int32, (GMLP_BLOCK, GMLP_BLOCK), 0) // CHUNK
    col_chunk = lax.broadcasted_iota(jnp.int32, (GMLP_BLOCK, GMLP_BLOCK), 1) // CHUNK
    causal = (col_chunk <= row_chunk).astype(F32)
    w_sp = [(ws_ref[hd] * causal).astype(BF16) for hd in range(N_HEADS_A)]
    first_head_tile = first_head[:, 0:LANES]
    for blk in range(ts // GMLP_BLOCK):
        rows = slice(blk * GMLP_BLOCK, (blk + 1) * GMLP_BLOCK)
        for p in range(N_PAIRS_A):
            cols = slice(p * LANES, (p + 1) * LANES)
            vp = va_scr[rows, cols]
            gate = jnp.where(first_head_tile,
                             _dot(w_sp[HEADS_PER_TILE * p], vp),
                             _dot(w_sp[HEADS_PER_TILE * p + 1], vp)) + bs_ref[:, cols]
            y_scr[rows, cols] = (u_scr[rows, cols] * gate).astype(BF16)

    key_pos = lax.broadcasted_iota(jnp.int32, (1, ATT_BAND), 1)
    chunks_per_tile = ts // CHUNK

    def chunk_body(c, carry):
        r0 = pl.multiple_of(c * CHUNK, CHUNK)
        first_valid_key = (ATT_LEFT_CHUNKS - (t * chunks_per_tile + c)) * CHUNK
        valid = key_pos >= first_valid_key
        for p in range(N_PAIRS_B):
            cols = slice(p * LANES, (p + 1) * LANES)
            qm = jnp.concatenate([q0_scr[pl.ds(r0, CHUNK), cols], q1_scr[pl.ds(r0, CHUNK), cols]], axis=0)
            kp = k_scr[pl.ds(r0, ATT_BAND), cols]
            vp = v_scr[pl.ds(r0, ATT_BAND), cols]
            s = jnp.where(valid, _dot_nt(qm, kp) + bias_ref[p], NEG_INF)
            e = jnp.exp(s - jnp.max(s, axis=-1, keepdims=True))
            o = _dot(e.astype(BF16), vp) / jnp.sum(e, axis=-1, keepdims=True)
            y_scr[pl.ds(r0, CHUNK), D_A + p * LANES:D_A + (p + 1) * LANES] = jnp.where(
                first_head_tile, o[0:CHUNK], o[CHUNK:2 * CHUNK]).astype(BF16)
        return carry

    lax.fori_loop(0, chunks_per_tile, chunk_body, 0)

    k_scr[0:ATT_HALO, :] = k_scr[ts:ts + ATT_HALO, :]
    v_scr[0:ATT_HALO, :] = v_scr[ts:ts + ATT_HALO, :]

    o_ref[...] = x_ref[...] + _dot(y_scr[...], w_out_ref[...])


def _attention_bias(b_rel):
    diag = np.arange(-(CHUNK - 1), ATT_BAND)
    idx = np.clip(ATT_HALO - diag, -REL_CLIP, REL_CLIP) + REL_CLIP
    cuts = [0] + [i for i in range(2, len(idx)) if idx[i] - idx[i - 1] != idx[i - 1] - idx[i - 2]] + [len(idx)]
    runs = []
    for lo, hi in zip(cuts[:-1], cuts[1:]):
        if hi - lo == 1 or idx[lo + 1] == idx[lo]:
            assert np.all(idx[lo:hi] == idx[lo])
            runs.append(jnp.broadcast_to(b_rel[:, idx[lo]:idx[lo] + 1], (b_rel.shape[0], hi - lo)))
        else:
            assert np.all(np.diff(idx[lo:hi]) == -1)
            runs.append(jnp.flip(b_rel[:, idx[hi - 1]:idx[lo] + 1], axis=1))
    diag_vals = jnp.concatenate(runs, axis=1).astype(F32)
    bias = jnp.stack([diag_vals[:, CHUNK - 1 - q:CHUNK - 1 - q + ATT_BAND] for q in range(CHUNK)], axis=1)
    return bias.reshape(N_PAIRS_B, HEADS_PER_TILE * CHUNK, ATT_BAND)


def _mixer_even(x, g, w_in, w_out, ln_g, ln_b, w_s, b_s, b_rel):
    bsz, seq, d = x.shape
    ts = SEQ_TILE
    assert seq % ts == 0 and ts % GMLP_BLOCK == 0 and ts >= ATT_HALO
    in_cols = w_in.shape[1]
    bs_rows = jnp.repeat(b_s.T, HEAD_DIM, axis=1).astype(F32)
    bias = _attention_bias(b_rel)
    x_spec = pl.BlockSpec((None, ts, d), lambda b, t: (b, t, 0))
    return pl.pallas_call(
        _mixer_even_kernel,
        grid=(bsz, seq // ts),
        in_specs=[
            x_spec,
            _const_spec((1, d)),
            _const_spec((d, in_cols)),
            _const_spec((D_A + D_B, d)),
            _const_spec((1, D_A)),
            _const_spec((1, D_A)),
            _const_spec((N_HEADS_A, GMLP_BLOCK, GMLP_BLOCK)),
            _const_spec((GMLP_BLOCK, D_A)),
            _const_spec((N_PAIRS_B, HEADS_PER_TILE * CHUNK, ATT_BAND)),
        ],
        out_specs=x_spec,
        out_shape=jax.ShapeDtypeStruct(x.shape, x.dtype),
        scratch_shapes=[
            pltpu.VMEM((ts, d), BF16),
            pltpu.VMEM((ts, D_A), F32),
            pltpu.VMEM((ts, D_A), BF16),
            pltpu.VMEM((ts, D_B), BF16),
            pltpu.VMEM((ts, D_B), BF16),
            pltpu.VMEM((ATT_HALO + ts, D_B), BF16),
            pltpu.VMEM((ATT_HALO + ts, D_B), BF16),
            pltpu.VMEM((ts, D_A + D_B), BF16),
        ],
        compiler_params=pltpu.CompilerParams(
            dimension_semantics=("arbitrary", "arbitrary"), vmem_limit_bytes=VMEM_LIMIT_BYTES),
        name="mixer_even",
    )(x, g.reshape(1, d), w_in.astype(BF16), w_out.astype(BF16), ln_g.reshape(1, D_A),
      ln_b.reshape(1, D_A), w_s, bs_rows, bias)


def _mixer_odd_kernel(x_ref, g_ref, w1_ref, b1_ref, wdw_ref, bdw_ref, ln_g_ref, ln_b_ref, w2_ref,
                      b2_ref, o_ref, h_scr, z_scr, c_scr, a_scr):
    t = pl.program_id(1)
    ts, d = x_ref.shape

    h_scr[...] = _rmsnorm(x_ref[...], g_ref[...]).astype(BF16)

    n_groups = ts // CONV_GROUP
    n_tiles = d // LANES

    @pl.when(t == 0)
    def _():
        z_scr[:, 0:CONV_GROUP, :] = jnp.zeros((n_tiles, CONV_GROUP, LANES), F32)

    for cb in range(d // GLU_COLS):
        cols = slice(cb * GLU_COLS, (cb + 1) * GLU_COLS)
        gcols = slice(d + cb * GLU_COLS, d + (cb + 1) * GLU_COLS)
        a = _dot(h_scr[...], w1_ref[:, cols]) + b1_ref[:, cols]
        gate = _dot(h_scr[...], w1_ref[:, gcols]) + b1_ref[:, gcols]
        z = a * jax.nn.sigmoid(gate)
        for jj in range(GLU_COLS // LANES):
            j = cb * (GLU_COLS // LANES) + jj
            for grp in range(n_groups):
                z_scr[j, CONV_PITCH * (grp + 1):CONV_PITCH * (grp + 1) + CONV_GROUP, :] = z[
                    grp * CONV_GROUP:(grp + 1) * CONV_GROUP, jj * LANES:(jj + 1) * LANES]

    for j in range(n_tiles):
        lanes = slice(j * LANES, (j + 1) * LANES)
        for ib in range(CONV_GROUP // CONV_OUT_BLOCK):
            accs = [jnp.zeros((SUBLANES, LANES), F32) for _ in range(CONV_OUT_BLOCK)]
            for k in range(CONV_WIDTH):
                wk = wdw_ref[k:k + 1, lanes]
                for o in range(CONV_OUT_BLOCK):
                    i = ib * CONV_OUT_BLOCK + o - (CONV_WIDTH - 1) + k
                    start = CONV_PITCH + i if i >= 0 else CONV_GROUP + i
                    accs[o] = accs[o] + z_scr[j, pl.ds(start, n_groups, stride=CONV_PITCH), :] * wk
            for o in range(CONV_OUT_BLOCK):
                c_scr[j, pl.ds(ib * CONV_OUT_BLOCK + o, n_groups, stride=CONV_PITCH), :] = (
                    accs[o] + bdw_ref[:, lanes])

    z_scr[:, 0:CONV_GROUP, :] = z_scr[:, CONV_PITCH * n_groups:CONV_PITCH * n_groups + CONV_GROUP, :]

    for grp in range(n_groups):
        conv = jnp.concatenate(
            [c_scr[j, CONV_PITCH * grp:CONV_PITCH * grp + CONV_GROUP, :] for j in range(n_tiles)], axis=1)
        zc = _layernorm(conv, ln_g_ref[...], ln_b_ref[...])
        a_scr[grp * CONV_GROUP:(grp + 1) * CONV_GROUP, :] = (zc * jax.nn.sigmoid(zc)).astype(BF16)

    o_ref[...] = x_ref[...] + _dot(a_scr[...], w2_ref[...]) + b2_ref[...]


def _mixer_odd(x, g, w1, b1, w_dw, b_dw, ln_g, ln_b, w2, b2):
    bsz, seq, d = x.shape
    ts = SEQ_TILE
    assert seq % ts == 0 and CONV_GROUP >= CONV_WIDTH - 1 and CONV_GROUP % CONV_OUT_BLOCK == 0
    assert d % GLU_COLS == 0 and GLU_COLS % LANES == 0
    n_tiles = d // LANES
    x_spec = pl.BlockSpec((None, ts, d), lambda b, t: (b, t, 0))
    return pl.pallas_call(
        _mixer_odd_kernel,
        grid=(bsz, seq // ts),
        in_specs=[
            x_spec,
            _const_spec((1, d)),
            _const_spec((d, 2 * d)),
            _const_spec((1, 2 * d)),
            _const_spec((CONV_WIDTH, d)),
            _const_spec((1, d)),
            _const_spec((1, d)),
            _const_spec((1, d)),
            _const_spec((d, d)),
            _const_spec((1, d)),
        ],
        out_specs=x_spec,
        out_shape=jax.ShapeDtypeStruct(x.shape, x.dtype),
        scratch_shapes=[
            pltpu.VMEM((ts, d), BF16),
            pltpu.VMEM((n_tiles, (SUBLANES + 1) * CONV_PITCH, LANES), F32),
            pltpu.VMEM((n_tiles, SUBLANES * CONV_PITCH, LANES), F32),
            pltpu.VMEM((ts, d), BF16),
        ],
        compiler_params=pltpu.CompilerParams(
            dimension_semantics=("arbitrary", "arbitrary"), vmem_limit_bytes=VMEM_LIMIT_BYTES),
        name="mixer_odd",
    )(x, g.reshape(1, d), w1.astype(BF16), b1.reshape(1, 2 * d), w_dw, b_dw.reshape(1, d),
      ln_g.reshape(1, d), ln_b.reshape(1, d), w2.astype(BF16), b2.reshape(1, d))


def _ffn_kernel(x_ref, g_ref, wg_ref, wu_ref, wd_ref, *rest, final_norm):
    if final_norm:
        gf_ref, o_ref, h_scr, a_scr = rest
    else:
        o_ref, h_scr, a_scr = rest
    d_ff = wg_ref.shape[1]

    h_scr[...] = _rmsnorm(x_ref[...], g_ref[...]).astype(BF16)
    for j in range(d_ff // FF_CHUNK):
        cols = slice(j * FF_CHUNK, (j + 1) * FF_CHUNK)
        gate = _dot(h_scr[...], wg_ref[:, cols])
        up = _dot(h_scr[...], wu_ref[:, cols])
        a_scr[:, cols] = (gate * jax.nn.sigmoid(gate) * up).astype(BF16)
    y = x_ref[...] + _dot(a_scr[...], wd_ref[...])
    if final_norm:
        y = _rmsnorm(y, gf_ref[...])
    o_ref[...] = y


def _ffn(x, g, wg, wu, wd, g_final=None):
    bsz, seq, d = x.shape
    d_ff = wg.shape[1]
    n_tok = bsz * seq
    tm = TOKEN_TILE
    assert n_tok % tm == 0 and d_ff % FF_CHUNK == 0
    x_spec = pl.BlockSpec((tm, d), lambda i: (i, 0))
    in_specs = [x_spec, _const_spec((1, d)), _const_spec((d, d_ff)), _const_spec((d, d_ff)),
                _const_spec((d_ff, d))]
    args = [x.reshape(n_tok, d), g.reshape(1, d), wg.astype(BF16), wu.astype(BF16), wd.astype(BF16)]
    if g_final is not None:
        in_specs.append(_const_spec((1, d)))
        args.append(g_final.reshape(1, d))
    out = pl.pallas_call(
        functools.partial(_ffn_kernel, final_norm=g_final is not None),
        grid=(n_tok // tm,),
        in_specs=in_specs,
        out_specs=x_spec,
        out_shape=jax.ShapeDtypeStruct((n_tok, d), x.dtype),
        scratch_shapes=[
            pltpu.VMEM((tm, d), BF16),
            pltpu.VMEM((tm, d_ff), BF16),
        ],
        compiler_params=pltpu.CompilerParams(
            dimension_semantics=("parallel",), vmem_limit_bytes=VMEM_LIMIT_BYTES),
        name="ffn_final" if g_final is not None else "ffn",
    )(*args)
    return out.reshape(bsz, seq, d)


def kernel(x, w_in, w_out, a_ln_g, a_ln_b, a_ws, a_bs, b_rel, c_w1, c_b1, c_dw, c_bdw, c_ln_g, c_ln_b,
           c_w2, c_b2, g_mix, g_ffn, f_wg, f_wu, f_wd, g_final):
    depth = g_mix.shape[0]
    for layer in range(depth):
        i = layer // 2
        if layer % 2 == 0:
            x = _mixer_even(x, g_mix[layer], w_in[i], w_out[i], a_ln_g[i], a_ln_b[i], a_ws[i], a_bs[i],
                            b_rel[i])
        else:
            x = _mixer_odd(x, g_mix[layer], c_w1[i], c_b1[i], c_dw[i], c_bdw[i], c_ln_g[i], c_ln_b[i],
                           c_w2[i], c_b2[i])
        x = _ffn(x, g_ffn[layer], f_wg[layer], f_wu[layer], f_wd[layer],
                 g_final if layer == depth - 1 else None)
    return x
```

```python
import functools

import jax
import jax.numpy as jnp
import numpy as np
from jax import lax
from jax.experimental import pallas as pl
from jax.experimental.pallas import tpu as pltpu

D_MODEL = 1024
CHUNK = 64
HEAD_DIM = 64
N_HEADS_A = 8
N_HEADS_B = 8
D_A = N_HEADS_A * HEAD_DIM
D_B = N_HEADS_B * HEAD_DIM
GMLP_BLOCK = 128
ATT_LEFT_CHUNKS = 8
ATT_HALO = ATT_LEFT_CHUNKS * CHUNK
ATT_BAND = ATT_HALO + CHUNK
ATT_Q = 2 * CHUNK
ATT_KEYS = ATT_HALO + ATT_Q
LOG2_E = 1.4426950408889634
REL_CLIP = 128
CONV_WIDTH = 31
EPS = 1e-6
NEG_INF = -1e30

LANES = 128
HEADS_PER_TILE = LANES // HEAD_DIM
N_PAIRS_A = D_A // LANES
N_PAIRS_B = D_B // LANES

SEQ_TILE = 512
TOKEN_TILE = 512
FF_CHUNK = 256
SUBLANES = 8
CONV_GROUP = SEQ_TILE // SUBLANES
CONV_PITCH = CONV_GROUP + SUBLANES
CONV_OUT_BLOCK = 8
GLU_COLS = 256
VMEM_LIMIT_BYTES = 56 * 1024 * 1024

BF16 = jnp.bfloat16
F32 = jnp.float32


def _dot(a, b):
    return jnp.dot(a, b, preferred_element_type=F32)


def _dot_nt(a, b):
    return lax.dot_general(a, b, (((1,), (1,)), ((), ())), preferred_element_type=F32)


def _rmsnorm(x, g):
    return x * lax.rsqrt(jnp.mean(x * x, axis=-1, keepdims=True) + EPS) * g


def _layernorm(x, g, b):
    mu = jnp.mean(x, axis=-1, keepdims=True)
    xc = x - mu
    var = jnp.mean(xc * xc, axis=-1, keepdims=True)
    return xc * lax.rsqrt(var + EPS) * g + b


def _const_spec(shape):
    return pl.BlockSpec(shape, lambda *_: (0,) * len(shape), pipeline_mode=pl.Buffered(1))


def _mixer_even_kernel(x_ref, g_ref, w_in_ref, w_out_ref, ln_g_ref, ln_b_ref, ws_ref, bs_ref,
                       bias_ref, o_ref,
                       h_scr, u_scr, va_scr, q0_scr, q1_scr, k_scr, v_scr, y_scr, s_scr, e_scr, l_scr):
    t = pl.program_id(1)
    ts = x_ref.shape[0]

    x = x_ref[...]
    h_scr[...] = _rmsnorm(x, g_ref[...]).astype(BF16)

    lane = lax.broadcasted_iota(jnp.int32, (1, D_B), 1)
    first_head = (lane % LANES) < HEAD_DIM

    u_scr[...] = jax.nn.gelu(_dot(h_scr[...], w_in_ref[:, 0:D_A]))
    va = jax.nn.gelu(_dot(h_scr[...], w_in_ref[:, D_A:2 * D_A]))
    va_scr[...] = _layernorm(va, ln_g_ref[...], ln_b_ref[...]).astype(BF16)
    q = (_dot(h_scr[...], w_in_ref[:, 2 * D_A:2 * D_A + D_B]) * (HEAD_DIM ** -0.5 * LOG2_E)).astype(BF16)
    zero = jnp.zeros_like(q)
    q0_scr[...] = jnp.where(first_head, q, zero)
    q1_scr[...] = jnp.where(first_head, zero, q)

    @pl.when(t == 0)
    def _():
        k_scr[0:ATT_HALO, :] = jnp.zeros((ATT_HALO, D_B), BF16)
        v_scr[0:ATT_HALO, :] = jnp.zeros((ATT_HALO, D_B), BF16)

    k_scr[ATT_HALO:ATT_HALO + ts, :] = _dot(
        h_scr[...], w_in_ref[:, 2 * D_A + D_B:2 * D_A + 2 * D_B]).astype(BF16)
    v_scr[ATT_HALO:ATT_HALO + ts, :] = _dot(
        h_scr[...], w_in_ref[:, 2 * D_A + 2 * D_B:2 * D_A + 3 * D_B]).astype(BF16)

    row_chunk = lax.broadcasted_iota(jnp.int32, (GMLP_BLOCK, GMLP_BLOCK), 0) // CHUNK
    col_chunk = lax.broadcasted_iota(jnp.int32, (GMLP_BLOCK, GMLP_BLOCK), 1) // CHUNK
    causal = (col_chunk <= row_chunk).astype(F32)
    w_sp = [(ws_ref[hd] * causal).astype(BF16) for hd in range(N_HEADS_A)]
    first_head_tile = first_head[:, 0:LANES]
    for blk in range(ts // GMLP_BLOCK):
        rows = slice(blk * GMLP_BLOCK, (blk + 1) * GMLP_BLOCK)
        for p in range(N_PAIRS_A):
            cols = slice(p * LANES, (p + 1) * LANES)
            vp = va_scr[rows, cols]
            gate = jnp.where(first_head_tile,
                             _dot(w_sp[HEADS_PER_TILE * p], vp),
                             _dot(w_sp[HEADS_PER_TILE * p + 1], vp)) + bs_ref[:, cols]
            y_scr[rows, cols] = (u_scr[rows, cols] * gate).astype(BF16)

    key_pos = lax.broadcasted_iota(jnp.int32, (1, ATT_KEYS), 1)
    n_blocks = ts // ATT_Q

    def block_rows(blk):
        return blk * ATT_Q if isinstance(blk, int) else pl.multiple_of(blk * ATT_Q, ATT_Q)

    def scores(blk, p):
        r0 = block_rows(blk)
        cols = slice(p * LANES, (p + 1) * LANES)
        qm = jnp.concatenate([q0_scr[pl.ds(r0, ATT_Q), cols], q1_scr[pl.ds(r0, ATT_Q), cols]], axis=0)
        first_valid_key = ATT_HALO - (t * ts + blk * ATT_Q)
        s = _dot_nt(qm, k_scr[pl.ds(r0, ATT_KEYS), cols]) + bias_ref[p]
        s_scr[p % 2] = jnp.where(key_pos >= first_valid_key, s, NEG_INF)

    def numerator(p):
        s = s_scr[p % 2]
        e = jnp.exp2(s - jnp.max(s, axis=-1, keepdims=True))
        e_scr[p % 2] = e.astype(BF16)
        l_scr[p % 2] = jnp.broadcast_to(jnp.sum(e, axis=-1, keepdims=True), (2 * ATT_Q, LANES))

    def weighted_values(blk, p):
        r0 = block_rows(blk)
        cols = slice(p * LANES, (p + 1) * LANES)
        o = _dot(e_scr[p % 2], v_scr[pl.ds(r0, ATT_KEYS), cols]) / l_scr[p % 2]
        y_scr[pl.ds(r0, ATT_Q), D_A + p * LANES:D_A + (p + 1) * LANES] = jnp.where(
            first_head_tile, o[0:ATT_Q], o[ATT_Q:2 * ATT_Q]).astype(BF16)

    def pipelined_block(blk, last):
        for p in range(N_PAIRS_B):
            ahead2, ahead1 = p + 2, p + 1
            if not (last and ahead2 >= N_PAIRS_B):
                scores(blk + ahead2 // N_PAIRS_B, ahead2 % N_PAIRS_B)
            if not (last and ahead1 >= N_PAIRS_B):
                numerator(ahead1 % N_PAIRS_B)
            weighted_values(blk, p)

    scores(0, 0)
    scores(0, 1)
    numerator(0)

    def block_body(blk, carry):
        pipelined_block(blk, False)
        return carry

    lax.fori_loop(0, n_blocks - 1, block_body, 0)
    pipelined_block(n_blocks - 1, True)

    k_scr[0:ATT_HALO, :] = k_scr[ts:ts + ATT_HALO, :]
    v_scr[0:ATT_HALO, :] = v_scr[ts:ts + ATT_HALO, :]

    o_ref[...] = x_ref[...] + _dot(y_scr[...], w_out_ref[...])


def _attention_bias(b_rel):
    diag = np.arange(-(ATT_Q - 1), ATT_KEYS)
    idx = np.clip(ATT_HALO - diag, -REL_CLIP, REL_CLIP) + REL_CLIP
    cuts = [0] + [i for i in range(2, len(idx)) if idx[i] - idx[i - 1] != idx[i - 1] - idx[i - 2]] + [len(idx)]
    runs = []
    for lo, hi in zip(cuts[:-1], cuts[1:]):
        if hi - lo == 1 or idx[lo + 1] == idx[lo]:
            assert np.all(idx[lo:hi] == idx[lo])
            runs.append(jnp.broadcast_to(b_rel[:, idx[lo]:idx[lo] + 1], (b_rel.shape[0], hi - lo)))
        else:
            assert np.all(np.diff(idx[lo:hi]) == -1)
            runs.append(jnp.flip(b_rel[:, idx[hi - 1]:idx[lo] + 1], axis=1))
    diag_vals = jnp.concatenate(runs, axis=1).astype(F32) * LOG2_E
    bias = jnp.stack([diag_vals[:, ATT_Q - 1 - q:ATT_Q - 1 - q + ATT_KEYS] for q in range(ATT_Q)], axis=1)
    band_start = (np.arange(ATT_Q) // CHUNK * CHUNK)[:, None]
    key = np.arange(ATT_KEYS)[None, :]
    in_band = (key >= band_start) & (key < band_start + ATT_BAND)
    bias = jnp.where(in_band[None], bias, NEG_INF)
    return bias.reshape(N_PAIRS_B, HEADS_PER_TILE * ATT_Q, ATT_KEYS)


def _mixer_even(x, g, w_in, w_out, ln_g, ln_b, w_s, b_s, b_rel):
    bsz, seq, d = x.shape
    ts = SEQ_TILE
    assert seq % ts == 0 and ts % GMLP_BLOCK == 0 and ts >= ATT_HALO and ts % ATT_Q == 0
    assert N_PAIRS_B % 2 == 0 and HEADS_PER_TILE == 2
    in_cols = w_in.shape[1]
    bs_rows = jnp.repeat(b_s.T, HEAD_DIM, axis=1).astype(F32)
    bias = _attention_bias(b_rel)
    x_spec = pl.BlockSpec((None, ts, d), lambda b, t: (b, t, 0))
    return pl.pallas_call(
        _mixer_even_kernel,
        grid=(bsz, seq // ts),
        in_specs=[
            x_spec,
            _const_spec((1, d)),
            _const_spec((d, in_cols)),
            _const_spec((D_A + D_B, d)),
            _const_spec((1, D_A)),
            _const_spec((1, D_A)),
            _const_spec((N_HEADS_A, GMLP_BLOCK, GMLP_BLOCK)),
            _const_spec((GMLP_BLOCK, D_A)),
            _const_spec((N_PAIRS_B, HEADS_PER_TILE * ATT_Q, ATT_KEYS)),
        ],
        out_specs=x_spec,
        out_shape=jax.ShapeDtypeStruct(x.shape, x.dtype),
        scratch_shapes=[
            pltpu.VMEM((ts, d), BF16),
            pltpu.VMEM((ts, D_A), F32),
            pltpu.VMEM((ts, D_A), BF16),
            pltpu.VMEM((ts, D_B), BF16),
            pltpu.VMEM((ts, D_B), BF16),
            pltpu.VMEM((ATT_HALO + ts, D_B), BF16),
            pltpu.VMEM((ATT_HALO + ts, D_B), BF16),
            pltpu.VMEM((ts, D_A + D_B), BF16),
            pltpu.VMEM((2, HEADS_PER_TILE * ATT_Q, ATT_KEYS), F32),
            pltpu.VMEM((2, HEADS_PER_TILE * ATT_Q, ATT_KEYS), BF16),
            pltpu.VMEM((2, HEADS_PER_TILE * ATT_Q, LANES), F32),
        ],
        compiler_params=pltpu.CompilerParams(
            dimension_semantics=("arbitrary", "arbitrary"), vmem_limit_bytes=VMEM_LIMIT_BYTES),
        name="mixer_even",
    )(x, g.reshape(1, d), w_in.astype(BF16), w_out.astype(BF16), ln_g.reshape(1, D_A),
      ln_b.reshape(1, D_A), w_s, bs_rows, bias)


def _mixer_odd_kernel(x_ref, g_ref, w1_ref, b1_ref, wdw_ref, bdw_ref, ln_g_ref, ln_b_ref, w2_ref,
                      b2_ref, o_ref, h_scr, z_scr, c_scr, a_scr):
    t = pl.program_id(1)
    ts, d = x_ref.shape

    h_scr[...] = _rmsnorm(x_ref[...], g_ref[...]).astype(BF16)

    n_groups = ts // CONV_GROUP
    n_tiles = d // LANES

    @pl.when(t == 0)
    def _():
        z_scr[:, 0:CONV_GROUP, :] = jnp.zeros((n_tiles, CONV_GROUP, LANES), F32)

    for cb in range(d // GLU_COLS):
        cols = slice(cb * GLU_COLS, (cb + 1) * GLU_COLS)
        gcols = slice(d + cb * GLU_COLS, d + (cb + 1) * GLU_COLS)
        a = _dot(h_scr[...], w1_ref[:, cols]) + b1_ref[:, cols]
        gate = _dot(h_scr[...], w1_ref[:, gcols]) + b1_ref[:, gcols]
        z = a * jax.nn.sigmoid(gate)
        for jj in range(GLU_COLS // LANES):
            j = cb * (GLU_COLS // LANES) + jj
            for grp in range(n_groups):
                z_scr[j, CONV_PITCH * (grp + 1):CONV_PITCH * (grp + 1) + CONV_GROUP, :] = z[
                    grp * CONV_GROUP:(grp + 1) * CONV_GROUP, jj * LANES:(jj + 1) * LANES]

    for j in range(n_tiles):
        lanes = slice(j * LANES, (j + 1) * LANES)
        for ib in range(CONV_GROUP // CONV_OUT_BLOCK):
            accs = [jnp.zeros((SUBLANES, LANES), F32) for _ in range(CONV_OUT_BLOCK)]
            for k in range(CONV_WIDTH):
                wk = wdw_ref[k:k + 1, lanes]
                for o in range(CONV_OUT_BLOCK):
                    i = ib * CONV_OUT_BLOCK + o - (CONV_WIDTH - 1) + k
                    start = CONV_PITCH + i if i >= 0 else CONV_GROUP + i
                    accs[o] = accs[o] + z_scr[j, pl.ds(start, n_groups, stride=CONV_PITCH), :] * wk
            for o in range(CONV_OUT_BLOCK):
                c_scr[j, pl.ds(ib * CONV_OUT_BLOCK + o, n_groups, stride=CONV_PITCH), :] = (
                    accs[o] + bdw_ref[:, lanes])

    z_scr[:, 0:CONV_GROUP, :] = z_scr[:, CONV_PITCH * n_groups:CONV_PITCH * n_groups + CONV_GROUP, :]

    for grp in range(n_groups):
        conv = jnp.concatenate(
            [c_scr[j, CONV_PITCH * grp:CONV_PITCH * grp + CONV_GROUP, :] for j in range(n_tiles)], axis=1)
        zc = _layernorm(conv, ln_g_ref[...], ln_b_ref[...])
        a_scr[grp * CONV_GROUP:(grp + 1) * CONV_GROUP, :] = (zc * jax.nn.sigmoid(zc)).astype(BF16)

    o_ref[...] = x_ref[...] + _dot(a_scr[...], w2_ref[...]) + b2_ref[...]


def _mixer_odd(x, g, w1, b1, w_dw, b_dw, ln_g, ln_b, w2, b2):
    bsz, seq, d = x.shape
    ts = SEQ_TILE
    assert seq % ts == 0 and CONV_GROUP >= CONV_WIDTH - 1 and CONV_GROUP % CONV_OUT_BLOCK == 0
    assert d % GLU_COLS == 0 and GLU_COLS % LANES == 0
    n_tiles = d // LANES
    x_spec = pl.BlockSpec((None, ts, d), lambda b, t: (b, t, 0))
    return pl.pallas_call(
        _mixer_odd_kernel,
        grid=(bsz, seq // ts),
        in_specs=[
            x_spec,
            _const_spec((1, d)),
            _const_spec((d, 2 * d)),
            _const_spec((1, 2 * d)),
            _const_spec((CONV_WIDTH, d)),
            _const_spec((1, d)),
            _const_spec((1, d)),
            _const_spec((1, d)),
            _const_spec((d, d)),
            _const_spec((1, d)),
        ],
        out_specs=x_spec,
        out_shape=jax.ShapeDtypeStruct(x.shape, x.dtype),
        scratch_shapes=[
            pltpu.VMEM((ts, d), BF16),
            pltpu.VMEM((n_tiles, (SUBLANES + 1) * CONV_PITCH, LANES), F32),
            pltpu.VMEM((n_tiles, SUBLANES * CONV_PITCH, LANES), F32),
            pltpu.VMEM((ts, d), BF16),
        ],
        compiler_params=pltpu.CompilerParams(
            dimension_semantics=("arbitrary", "arbitrary"), vmem_limit_bytes=VMEM_LIMIT_BYTES),
        name="mixer_odd",
    )(x, g.reshape(1, d), w1.astype(BF16), b1.reshape(1, 2 * d), w_dw, b_dw.reshape(1, d),
      ln_g.reshape(1, d), ln_b.reshape(1, d), w2.astype(BF16), b2.reshape(1, d))


def _ffn_kernel(x_ref, g_ref, wg_ref, wu_ref, wd_ref, *rest, final_norm):
    if final_norm:
        gf_ref, o_ref, h_scr, a_scr = rest
    else:
        o_ref, h_scr, a_scr = rest
    d_ff = wg_ref.shape[1]

    h_scr[...] = _rmsnorm(x_ref[...], g_ref[...]).astype(BF16)
    for j in range(d_ff // FF_CHUNK):
        cols = slice(j * FF_CHUNK, (j + 1) * FF_CHUNK)
        gate = _dot(h_scr[...], wg_ref[:, cols])
        up = _dot(h_scr[...], wu_ref[:, cols])
        a_scr[:, cols] = (gate * jax.nn.sigmoid(gate) * up).astype(BF16)
    y = x_ref[...] + _dot(a_scr[...], wd_ref[...])
    if final_norm:
        y = _rmsnorm(y, gf_ref[...])
    o_ref[...] = y


def _ffn(x, g, wg, wu, wd, g_final=None):
    bsz, seq, d = x.shape
    d_ff = wg.shape[1]
    n_tok = bsz * seq
    tm = TOKEN_TILE
    assert n_tok % tm == 0 and d_ff % FF_CHUNK == 0
    x_spec = pl.BlockSpec((tm, d), lambda i: (i, 0))
    in_specs = [x_spec, _const_spec((1, d)), _const_spec((d, d_ff)), _const_spec((d, d_ff)),
                _const_spec((d_ff, d))]
    args = [x.reshape(n_tok, d), g.reshape(1, d), wg.astype(BF16), wu.astype(BF16), wd.astype(BF16)]
    if g_final is not None:
        in_specs.append(_const_spec((1, d)))
        args.append(g_final.reshape(1, d))
    out = pl.pallas_call(
        functools.partial(_ffn_kernel, final_norm=g_final is not None),
        grid=(n_tok // tm,),
        in_specs=in_specs,
        out_specs=x_spec,
        out_shape=jax.ShapeDtypeStruct((n_tok, d), x.dtype),
        scratch_shapes=[
            pltpu.VMEM((tm, d), BF16),
            pltpu.VMEM((tm, d_ff), BF16),
        ],
        compiler_params=pltpu.CompilerParams(
            dimension_semantics=("parallel",), vmem_limit_bytes=VMEM_LIMIT_BYTES),
        name="ffn_final" if g_final is not None else "ffn",
    )(*args)
    return out.reshape(bsz, seq, d)


def kernel(x, w_in, w_out, a_ln_g, a_ln_b, a_ws, a_bs, b_rel, c_w1, c_b1, c_dw, c_bdw, c_ln_g, c_ln_b,
           c_w2, c_b2, g_mix, g_ffn, f_wg, f_wu, f_wd, g_final):
    depth = g_mix.shape[0]
    for layer in range(depth):
        i = layer // 2
        if layer % 2 == 0:
            x = _mixer_even(x, g_mix[layer], w_in[i], w_out[i], a_ln_g[i], a_ln_b[i], a_ws[i], a_bs[i],
                            b_rel[i])
        else:
            x = _mixer_odd(x, g_mix[layer], c_w1[i], c_b1[i], c_dw[i], c_bdw[i], c_ln_g[i], c_ln_b[i],
                           c_w2[i], c_b2[i])
        x = _ffn(x, g_ffn[layer], f_wg[layer], f_wu[layer], f_wd[layer],
                 g_final if layer == depth - 1 else None)
    return x
```
